```python
import math
import jax
import jax.numpy as jnp
from jax import lax
import numpy as np

D_MODEL = 2048
BATCH = 8
SEQ = 4096
DEPTH = 2

GRID_W = 64
CTX_LEN = 256
EPS = 1e-6
NEG_INF = -1e30

NA_HEADS = 16
NA_HEAD_DIM = 64
NA_WIDTH = NA_HEADS * NA_HEAD_DIM
NA_KH = 8
NA_KW = 16

S5_WIDTH = D_MODEL - NA_WIDTH
S5_GROUP = 16
S5_GROUPS = S5_WIDTH // S5_GROUP
S5_STATE = 64
EVEN_IN = 3 * NA_WIDTH + S5_WIDTH

HG_HEADS = 16
HG_DK = D_MODEL // HG_HEADS
HG_CHUNK = 32
ODD_IN = 4 * D_MODEL
ROPE_BASE = 10000.0

N_EXPERTS = 32
TOP_K = 4
D_FF = 1024
SWIGLU_LIMIT = 7.0
SWIGLU_ALPHA = 1.702

N_EVEN = (DEPTH + 1) // 2
N_ODD = DEPTH // 2

kernel_name = 'hybrid_natten_s5_hgrn2_moe_dit'


def rmsnorm(x, w):
    x32 = x.astype(jnp.float32)
    y = x32 * lax.rsqrt(jnp.mean(x32 * x32, axis=-1, keepdims=True) + EPS)
    return (y * w.astype(jnp.float32)).astype(x.dtype)


def modulate(h, shift, scale):
    return h * (1.0 + scale) + shift


def axial_rope(seq_len, dim):
    t = jnp.arange(seq_len)
    row = (t // GRID_W).astype(jnp.float32)
    col = (t % GRID_W).astype(jnp.float32)
    n = dim // 4
    inv = ROPE_BASE ** (-jnp.arange(n, dtype=jnp.float32) / n)
    ang = jnp.concatenate([row[:, None] * inv, col[:, None] * inv], axis=-1)
    return jnp.cos(ang), jnp.sin(ang)


def apply_rope(x, cos, sin):
    x1, x2 = x[..., 0::2], x[..., 1::2]
    return jnp.stack([x1 * cos - x2 * sin, x1 * sin + x2 * cos], axis=-1).reshape(x.shape)


def neighbourhood_attention(q, k, v, k_ctx, v_ctx, rpb):
    B, S, H, dh = q.shape
    R, W = S // GRID_W, GRID_W
    kh, kw = min(NA_KH, R), NA_KW
    qg = q.reshape(B, R, W, H, dh) * (dh ** -0.5)
    kg = k.reshape(B, R, W, H, dh)
    vg = v.reshape(B, R, W, H, dh)
    r = jnp.arange(R)
    rows = jnp.clip(r - kh // 2, 0, R - kh)[:, None] + jnp.arange(kh)[None, :]
    k_band = kg[:, rows]
    v_band = vg[:, rows]
    cq = jnp.arange(W)
    c0 = jnp.clip(cq - kw // 2, 0, W - kw)
    col_ok = (cq[None, :] >= c0[:, None]) & (cq[None, :] < c0[:, None] + kw)
    ro = rows - r[:, None] + (NA_KH - 1)
    co = jnp.clip(cq[None, :] - cq[:, None], -(kw - 1), kw - 1) + (kw - 1)
    bias = rpb[:, ro[:, None, :, None], co[None, :, None, :]].astype(jnp.float32)
    s_band = jnp.einsum('brqhd,brkxhd->bhrqkx', qg, k_band).astype(jnp.float32) + bias
    s_band = jnp.where(col_ok[:, None, :], s_band, NEG_INF)
    s_ctx = jnp.einsum('brqhd,bjhd->bhrqj', qg, k_ctx).astype(jnp.float32)
    n_band = kh * W
    s = jnp.concatenate([s_band.reshape(B, H, R, W, n_band), s_ctx], axis=-1)
    p = jax.nn.softmax(s, axis=-1).astype(v.dtype)
    p_band = p[..., :n_band].reshape(B, H, R, W, kh, W)
    o = (jnp.einsum('bhrqkx,brkxhd->brqhd', p_band, v_band)
         + jnp.einsum('bhrqj,bjhd->brqhd', p[..., n_band:], v_ctx))
    return o.reshape(B, S, H * dh)


def context_attention(q, k, v):
    dh = q.shape[-1]
    s = jnp.einsum('bqhd,bkhd->bhqk', q * (dh ** -0.5), k).astype(jnp.float32)
    p = jax.nn.softmax(s, axis=-1).astype(v.dtype)
    o = jnp.einsum('bhqk,bkhd->bqhd', p, v)
    return o.reshape(q.shape[0], q.shape[1], -1)


def _linear_combine(e1, e2):
    a1, b1 = e1
    a2, b2 = e2
    return a1 * a2, a2 * b1 + b2


def s5_discretise(a_re, a_im, b_re, b_im, log_dt):
    lam = lax.complex(a_re.astype(jnp.float32), a_im.astype(jnp.float32))
    dt = jnp.exp(log_dt.astype(jnp.float32))[:, None]
    a_bar = jnp.exp(lam * dt)
    b_cplx = lax.complex(b_re.astype(jnp.float32), b_im.astype(jnp.float32))
    b_bar = ((a_bar - 1.0) / lam)[..., None] * b_cplx
    return a_bar, b_bar


def s5_scan(u, a_bar, b_bar, h0, reverse):
    bu = jnp.einsum('gpc,btgc->btgp', b_bar, u.astype(jnp.complex64))
    edge = -1 if reverse else 0
    bu = bu.at[:, edge].add(a_bar * h0)
    a = jnp.broadcast_to(a_bar, (1,) + bu.shape[1:])
    _, h = lax.associative_scan(_linear_combine, (a, bu), reverse=reverse, axis=1)
    final = h[:, 0] if reverse else h[:, -1]
    return h, final


def s5_readout(c_re, c_im, h):
    cm = lax.complex(c_re.astype(jnp.float32), c_im.astype(jnp.float32))
    y = jnp.real(jnp.einsum('gcp,btgp->btgc', cm, h))
    return y.reshape(y.shape[0], y.shape[1], -1)


def even_mixer(hl, hc, w_in, w_out, rpb, a_re, a_im, b_re, b_im, c_re, c_im, log_dt, d_skip, w_glu, need_ctx):
    B, S, _ = hl.shape
    Lc = hc.shape[1]

    def project(h):
        T = h.shape[1]
        p = h @ w_in
        q, k, v, u = jnp.split(p, [NA_WIDTH, 2 * NA_WIDTH, 3 * NA_WIDTH], axis=-1)
        hd = (B, T, NA_HEADS, NA_HEAD_DIM)
        return (q.reshape(hd), k.reshape(hd), v.reshape(hd),
                u.astype(jnp.float32).reshape(B, T, S5_GROUPS, S5_GROUP))

    ql, kl, vl, ul = project(hl)
    qc, kc, vc, uc = project(hc)
    att_l = neighbourhood_attention(ql, kl, vl, kc, vc, rpb)

    d32 = d_skip.astype(jnp.float32)
    y_l = d32 * ul.reshape(B, S, S5_WIDTH)
    y_c = d32 * uc.reshape(B, Lc, S5_WIDTH) if need_ctx else None
    h0 = jnp.zeros((B, S5_GROUPS, S5_STATE), jnp.complex64)
    for dirn, reverse in ((0, False), (1, True)):
        a_bar, b_bar = s5_discretise(a_re[dirn], a_im[dirn], b_re[dirn], b_im[dirn], log_dt[dirn])
        h_c, h_c_final = s5_scan(uc, a_bar, b_bar, h0, reverse)
        h_l, _ = s5_scan(ul, a_bar, b_bar, h_c_final, reverse)
        y_l = y_l + s5_readout(c_re[dirn], c_im[dirn], h_l)
        if need_ctx:
            y_c = y_c + s5_readout(c_re[dirn], c_im[dirn], h_c)

    def merge(att, y):
        z = jax.nn.gelu(y)
        z = z * jax.nn.sigmoid(z @ w_glu)
        return jnp.concatenate([att, z.astype(att.dtype)], axis=-1) @ w_out

    out_l = merge(att_l, y_l)
    if not need_ctx:
        return out_l, None
    return out_l, merge(context_attention(qc, kc, vc), y_c)


def chunk_gla(q, k, v, log_f, s0, with_output=True):
    B, H, T, dk = k.shape
    dv = v.shape[-1]
    n, C = T // HG_CHUNK, HG_CHUNK
    k = k.reshape(B, H, n, C, dk)
    v = v.reshape(B, H, n, C, dv)
    b = jnp.cumsum(log_f.reshape(B, H, n, C, dk), axis=3)
    b_end = b[:, :, :, -1:]
    ds = jnp.einsum('bhncd,bhncv->nbhdv', k * jnp.exp(b_end - b), v)
    dec = jnp.moveaxis(jnp.exp(b_end[:, :, :, 0]), 2, 0)

    def step(s, xs):
        d, dsn = xs
        return d[..., None] * s + dsn, s

    s_final, s_start = lax.scan(step, s0, (dec, ds))
    if not with_output:
        return None, s_final
    qb = q.reshape(B, H, n, C, dk) * jnp.exp(b)
    att = jnp.einsum('bhncd,bhnsd->bhncs', qb, k * jnp.exp(-b))
    att = jnp.where(jnp.tril(jnp.ones((C, C), dtype=bool)), att, 0.0)
    o = jnp.einsum('bhncs,bhnsv->bhncv', att, v) + jnp.einsum('bhncd,nbhdv->bhncv', qb, s_start)
    return o.reshape(B, H, T, dv), s_final


def odd_mixer(hl, hc, w_in, w_out, lb, g_norm_w, cos, sin, need_ctx):
    D, H, dk = D_MODEL, HG_HEADS, HG_DK
    B = hl.shape[0]

    def heads(t):
        return t.reshape(t.shape[0], t.shape[1], H, -1).transpose(0, 2, 1, 3)

    def project(h):
        p = (h @ w_in).astype(jnp.float32)
        q, f_fwd, f_bwd, i, g = jnp.split(p, [D, D + D // 2, 2 * D, 3 * D], axis=-1)
        return heads(q) * (dk ** -0.5), (f_fwd, f_bwd), heads(i), g

    def key_and_decay(f_logit):
        f = lb + (1.0 - lb) * jax.nn.sigmoid(f_logit)
        f = heads(jnp.repeat(f, 2, axis=-1))
        return 1.0 - f, jnp.log(f)

    ql, fl, vl, gl = project(hl)
    qc, fc, vc, gc = project(hc)
    ql = apply_rope(ql, cos, sin)
    s0 = jnp.zeros((B, H, dk, dk), jnp.float32)
    outs_l, outs_c = [], []
    for dirn in (0, 1):
        flip = (lambda t: jnp.flip(t, axis=2)) if dirn == 1 else (lambda t: t)
        kc, logfc = key_and_decay(fc[dirn])
        kl, logfl = key_and_decay(fl[dirn])
        kl = apply_rope(kl, cos, sin)
        oc, sc = chunk_gla(flip(qc) if need_ctx else None, flip(kc), flip(vc), flip(logfc), s0,
                           with_output=need_ctx)
        ol, _ = chunk_gla(flip(ql), flip(kl), flip(vl), flip(logfl), sc)
        outs_l.append(flip(ol))
        if need_ctx:
            outs_c.append(flip(oc))

    def readout(o, g):
        o = o * lax.rsqrt(jnp.mean(o * o, axis=-1, keepdims=True) + EPS)
        o = o.transpose(0, 2, 1, 3).reshape(g.shape) * g_norm_w.astype(jnp.float32) * jax.nn.silu(g)
        return o.astype(hl.dtype) @ w_out

    out_l = readout(outs_l[0] + outs_l[1], gl)
    if not need_ctx:
        return out_l, None
    return out_l, readout(outs_c[0] + outs_c[1], gc)


def moe_ffn(h, router_w, router_b, w1, b1, w2, b2):
    logits = (h @ router_w + router_b).astype(jnp.float32)
    top_val, top_idx = lax.top_k(logits, TOP_K)
    gates = jax.nn.softmax(top_val, axis=-1)
    combine = jnp.einsum('nk,nke->ne', gates, jax.nn.one_hot(top_idx, N_EXPERTS, dtype=jnp.float32))
    out = jnp.zeros(h.shape, jnp.float32)
    for e in range(N_EXPERTS):
        hu = h @ w1[e] + b1[e]
        x_glu = jnp.minimum(hu[:, 0::2], SWIGLU_LIMIT)
        x_lin = jnp.clip(hu[:, 1::2], -SWIGLU_LIMIT, SWIGLU_LIMIT)
        act = x_glu * jax.nn.sigmoid(SWIGLU_ALPHA * x_glu) * (x_lin + 1.0)
        out = out + combine[:, e:e + 1] * (act @ w2[e] + b2[e])
    return out.astype(h.dtype)


def setup_inputs(seed: int = 0) -> dict:
    key = jax.random.key(seed)
    keys = iter(jax.random.split(key, 32))

    def normal(shape, scale):
        return jax.random.normal(next(keys), shape, jnp.float32) * scale

    D = D_MODEL
    gp = (N_EVEN, 2, S5_GROUPS)
    return {
        'x': normal((BATCH, SEQ, D), 1.0),
        'c': normal((BATCH, D), 1.0),
        'ctx': normal((BATCH, CTX_LEN, D), 1.0),
        'c_ctx': normal((D,), 1.0),
        'w_mod': normal((DEPTH, D, 6 * D), 0.5 * D ** -0.5),
        'b_mod': normal((DEPTH, 6 * D), 0.02),
        'norm_w': 1.0 + normal((DEPTH, 4, D), 0.05),
        'ev_w_in': normal((N_EVEN, D, EVEN_IN), D ** -0.5),
        'ev_w_out': normal((N_EVEN, D, D), D ** -0.5),
        'na_rpb': normal((N_EVEN, NA_HEADS, 2 * NA_KH - 1, 2 * NA_KW - 1), 0.1),
        's5_a_re': -0.5 + normal(gp + (S5_STATE,), 0.01),
        's5_a_im': math.pi * jnp.arange(S5_STATE, dtype=jnp.float32) + normal(gp + (S5_STATE,), 0.01),
        's5_b_re': normal(gp + (S5_STATE, S5_GROUP), (2 * S5_GROUP) ** -0.5),
        's5_b_im': normal(gp + (S5_STATE, S5_GROUP), (2 * S5_GROUP) ** -0.5),
        's5_c_re': normal(gp + (S5_GROUP, S5_STATE), (2 * S5_STATE) ** -0.5),
        's5_c_im': normal(gp + (S5_GROUP, S5_STATE), (2 * S5_STATE) ** -0.5),
        's5_log_dt': jax.random.uniform(next(keys), gp, jnp.float32, math.log(1e-3), math.log(1e-1)),
        's5_d': normal((N_EVEN, S5_WIDTH), 1.0),
        's5_w_glu': normal((N_EVEN, S5_WIDTH, S5_WIDTH), S5_WIDTH ** -0.5),
        'od_w_in': normal((N_ODD, D, ODD_IN), D ** -0.5),
        'od_w_out': normal((N_ODD, D, D), D ** -0.5),
        'hg_lb_logit': 1.0 + normal((DEPTH, D // 2), 0.1),
        'hg_norm_w': 1.0 + normal((N_ODD, D), 0.05),
        'router_w': normal((DEPTH, D, N_EXPERTS), D ** -0.5),
        'router_b': normal((DEPTH, N_EXPERTS), 0.01),
        'exp_w1': normal((DEPTH, N_EXPERTS, D, 2 * D_FF), D ** -0.5),
        'exp_b1': normal((DEPTH, N_EXPERTS, 2 * D_FF), 0.01),
        'exp_w2': normal((DEPTH, N_EXPERTS, D_FF, D), D_FF ** -0.5),
        'exp_b2': normal((DEPTH, N_EXPERTS, D), 0.01),
    }


def reference(x, c, ctx, c_ctx, w_mod, b_mod, norm_w, ev_w_in, ev_w_out, na_rpb, s5_a_re, s5_a_im,
              s5_b_re, s5_b_im, s5_c_re, s5_c_im, s5_log_dt, s5_d, s5_w_glu, od_w_in, od_w_out,
              hg_lb_logit, hg_norm_w, router_w, router_b, exp_w1, exp_b1, exp_w2, exp_b2):
    B, S, D = x.shape
    Lc = ctx.shape[1]
    cos, sin = axial_rope(S, HG_DK)
    lb_all = jnp.cumsum(jax.nn.softmax(hg_lb_logit.astype(jnp.float32), axis=0), axis=0)
    lb_all = lb_all - lb_all[0]
    xl, xc = x, ctx
    for l in range(DEPTH):
        last = l == DEPTH - 1
        mod_l = (jax.nn.silu(c) @ w_mod[l] + b_mod[l]).reshape(B, 6, 1, D)
        mod_c = (jax.nn.silu(c_ctx) @ w_mod[l] + b_mod[l]).reshape(6, D)
        hl = modulate(rmsnorm(xl, norm_w[l, 0]), mod_l[:, 0], mod_l[:, 1])
        hc = modulate(rmsnorm(xc, norm_w[l, 0]), mod_c[0], mod_c[1])
        if l % 2 == 0:
            e = l // 2
            yl, yc = even_mixer(hl, hc, ev_w_in[e], ev_w_out[e], na_rpb[e], s5_a_re[e], s5_a_im[e],
                                s5_b_re[e], s5_b_im[e], s5_c_re[e], s5_c_im[e], s5_log_dt[e],
                                s5_d[e], s5_w_glu[e], not last)
        else:
            o = l // 2
            yl, yc = odd_mixer(hl, hc, od_w_in[o], od_w_out[o], lb_all[l], hg_norm_w[o], cos, sin, not last)
        xl = xl + mod_l[:, 2] * rmsnorm(yl, norm_w[l, 1])
        hl2 = modulate(rmsnorm(xl, norm_w[l, 2]), mod_l[:, 3], mod_l[:, 4])
        if last:
            y2 = moe_ffn(hl2.reshape(B * S, D), router_w[l], router_b[l], exp_w1[l], exp_b1[l],
                         exp_w2[l], exp_b2[l]).reshape(B, S, D)
            xl = xl + mod_l[:, 5] * rmsnorm(y2, norm_w[l, 3])
        else:
            xc = xc + mod_c[2] * rmsnorm(yc, norm_w[l, 1])
            hc2 = modulate(rmsnorm(xc, norm_w[l, 2]), mod_c[3], mod_c[4])
            h2 = jnp.concatenate([hl2, hc2], axis=1).reshape(B * (S + Lc), D)
            y2 = moe_ffn(h2, router_w[l], router_b[l], exp_w1[l], exp_b1[l], exp_w2[l], exp_b2[l])
            y2 = rmsnorm(y2, norm_w[l, 3]).reshape(B, S + Lc, D)
            xl = xl + mod_l[:, 5] * y2[:, :S]
            xc = xc + mod_c[5] * y2[:, S:]
    return xl
```

```python
import functools
import math

import jax
import jax.numpy as jnp
from jax import lax
from jax.experimental import pallas as pl
from jax.experimental.pallas import tpu as pltpu

F32 = jnp.float32
BF16 = jnp.bfloat16

GRID_W = 64
EPS = 1e-6
NEG_INF = -1e30

NA_HEAD_DIM = 64
NA_KH = 8
NA_KW = 16

S5_GROUP = 16
S5_CHUNK = 16

HG_HEADS = 16
HG_CHUNK = 32
HG_TILE = 256
ROPE_BASE = 10000.0

TOP_K = 4
SWIGLU_LIMIT = 7.0
SWIGLU_ALPHA = 1.702

VMEM_LIMIT_BYTES = 56 * 1024 * 1024
LANES = 128

ROW_TILE = 512
INPROJ_TM = 1024
INPROJ_TN = 1024
EXPERT_TM = 256


def _cparams(sem):
    return pltpu.CompilerParams(dimension_semantics=sem, vmem_limit_bytes=VMEM_LIMIT_BYTES)


def _rms(x):
    return x * lax.rsqrt(jnp.mean(x * x, axis=-1, keepdims=True) + EPS)


def _mod_kernel(c_ref, w_ref, b_ref, o_ref):
    c = c_ref[...]
    a = (c * jax.nn.sigmoid(c)).astype(BF16)
    o_ref[...] = jnp.dot(a, w_ref[...].astype(BF16), preferred_element_type=F32) + b_ref[...]


def _mod_vectors(cc, w, b):
    rows, d = cc.shape
    n = w.shape[1]
    tn = min(1024, n)
    return pl.pallas_call(
        _mod_kernel,
        out_shape=jax.ShapeDtypeStruct((rows, n), F32),
        grid=(n // tn,),
        in_specs=[pl.BlockSpec((rows, d), lambda j: (0, 0)),
                  pl.BlockSpec((d, tn), lambda j: (0, j)),
                  pl.BlockSpec((1, tn), lambda j: (0, j))],
        out_specs=pl.BlockSpec((rows, tn), lambda j: (0, j)),
        compiler_params=_cparams(("parallel",)),
        name="mod_vectors",
    )(cc, w, b.reshape(1, n))


def _inproj_kernel(x_ref, mod_ref, nw_ref, w_ref, o_ref, h_scr):
    @pl.when(pl.program_id(1) == 0)
    def _():
        y = _rms(x_ref[...]) * nw_ref[...]
        h_scr[...] = (y * (1.0 + mod_ref[0, 1:2, :]) + mod_ref[0, 0:1, :]).astype(BF16)

    o_ref[...] = jnp.dot(h_scr[...], w_ref[...], preferred_element_type=F32).astype(o_ref.dtype)


def _inproj(x, mod, nw, w, mod_row):
    nt, d = x.shape
    n = w.shape[1]
    tm = min(INPROJ_TM, nt)
    tn = min(INPROJ_TN, n)
    return pl.pallas_call(
        _inproj_kernel,
        out_shape=jax.ShapeDtypeStruct((nt, n), BF16),
        grid=(nt // tm, n // tn),
        in_specs=[pl.BlockSpec((tm, d), lambda i, j: (i, 0)),
                  pl.BlockSpec((1, 6, d), lambda i, j: (mod_row(i, tm), 0, 0)),
                  pl.BlockSpec((1, d), lambda i, j: (0, 0)),
                  pl.BlockSpec((d, tn), lambda i, j: (0, j))],
        out_specs=pl.BlockSpec((tm, tn), lambda i, j: (i, j)),
        scratch_shapes=[pltpu.VMEM((tm, d), BF16)],
        compiler_params=_cparams(("parallel", "arbitrary")),
        name="inproj",
    )(x, mod, nw, w)


def _na_kernel(q_ref, k_ref, v_ref, kc_ref, vc_ref, z_ref, o_ref, *, rows_total, rows_per_step):
    i = pl.program_id(2)
    w = GRID_W
    lane = lax.broadcasted_iota(jnp.int32, (w, LANES), 1)
    head_lo = lane < NA_HEAD_DIM
    kc = kc_ref[...]
    vc = vc_ref[...]
    scale = NA_HEAD_DIM ** -0.5
    nt_dims = (((1,), (1,)), ((), ()))

    def row_body(rl, carry):
        r = i * rows_per_step + rl
        w0 = jnp.clip(r - NA_KH // 2, 0, rows_total - NA_KH)
        base = (NA_KH - 1) - (r - w0)
        q0 = pl.multiple_of(rl * w, w)
        k0 = pl.multiple_of(w0 * w, w)
        qrow = q_ref[pl.ds(q0, w), :]
        kwin = k_ref[pl.ds(k0, NA_KH * w), :]
        vwin = v_ref[pl.ds(k0, NA_KH * w), :]
        outs = []
        for h in range(2):
            sel = head_lo if h == 0 else jnp.logical_not(head_lo)
            qh = jnp.where(sel, qrow, jnp.zeros_like(qrow))
            s_band = lax.dot_general(qh, kwin, nt_dims, preferred_element_type=F32) * scale
            bias = jnp.concatenate([z_ref[h, base + 2 * jj] for jj in range(NA_KH // 2)], axis=-1)
            s_band = s_band + bias
            s_ctx = lax.dot_general(qh, kc, nt_dims, preferred_element_type=F32) * scale
            m = jnp.maximum(jnp.max(s_band, axis=-1, keepdims=True),
                            jnp.max(s_ctx, axis=-1, keepdims=True))
            p_band = jnp.exp(s_band - m)
            p_ctx = jnp.exp(s_ctx - m)
            denom = jnp.sum(p_band, axis=-1, keepdims=True) + jnp.sum(p_ctx, axis=-1, keepdims=True)
            o = (jnp.dot(p_band.astype(BF16), vwin, preferred_element_type=F32)
                 + jnp.dot(p_ctx.astype(BF16), vc, preferred_element_type=F32))
            outs.append(o / denom)
        o_ref[pl.ds(q0, w), :] = jnp.where(head_lo, outs[0], outs[1]).astype(o_ref.dtype)
        return carry

    lax.fori_loop(0, rows_per_step, row_body, 0)


def _na_bias_table(rpb):
    w = GRID_W
    cq = jnp.arange(w)
    c0 = jnp.clip(cq - NA_KW // 2, 0, w - NA_KW)
    col_ok = (cq[None, :] >= c0[:, None]) & (cq[None, :] < c0[:, None] + NA_KW)
    co = jnp.clip(cq[None, :] - cq[:, None], -(NA_KW - 1), NA_KW - 1) + (NA_KW - 1)
    z = rpb.astype(F32)[:, :, co]
    z = jnp.where(col_ok[None, None], z, NEG_INF)
    return jnp.concatenate([z[:, :-1], z[:, 1:]], axis=-1)


def _na_latent(p, ztab, batch, seq, ctx_len, n_lat):
    naw = ztab.shape[0] * NA_HEAD_DIM
    hp = naw // LANES
    rows_total = seq // GRID_W
    rps = min(8, rows_total)
    tq = rps * GRID_W
    ctx_blk0 = n_lat // ctx_len
    kern = functools.partial(_na_kernel, rows_total=rows_total, rows_per_step=rps)
    return pl.pallas_call(
        kern,
        out_shape=jax.ShapeDtypeStruct((n_lat, naw), BF16),
        grid=(batch, hp, rows_total // rps),
        in_specs=[pl.BlockSpec((tq, LANES), lambda b, g, i: (b * (seq // tq) + i, g)),
                  pl.BlockSpec((seq, LANES), lambda b, g, i: (b, hp + g)),
                  pl.BlockSpec((seq, LANES), lambda b, g, i: (b, 2 * hp + g)),
                  pl.BlockSpec((ctx_len, LANES), lambda b, g, i: (ctx_blk0 + b, hp + g)),
                  pl.BlockSpec((ctx_len, LANES), lambda b, g, i: (ctx_blk0 + b, 2 * hp + g)),
                  pl.BlockSpec((2, 2 * NA_KH - 2, GRID_W, 2 * GRID_W), lambda b, g, i: (g, 0, 0, 0))],
        out_specs=pl.BlockSpec((tq, LANES), lambda b, g, i: (b * (seq // tq) + i, g)),
        compiler_params=_cparams(("parallel", "parallel", "parallel")),
        name="na_latent",
    )(p, p, p, p, p, ztab)


def _ctx_attn_kernel(q_ref, k_ref, v_ref, o_ref):
    lane = lax.broadcasted_iota(jnp.int32, q_ref.shape, 1)
    head_lo = lane < NA_HEAD_DIM
    q = q_ref[...]
    k = k_ref[...]
    v = v_ref[...]
    outs = []
    for h in range(2):
        sel = head_lo if h == 0 else jnp.logical_not(head_lo)
        qh = jnp.where(sel, q, jnp.zeros_like(q))
        s = lax.dot_general(qh, k, (((1,), (1,)), ((), ())), preferred_element_type=F32)
        s = s * (NA_HEAD_DIM ** -0.5)
        pr = jnp.exp(s - jnp.max(s, axis=-1, keepdims=True))
        o = jnp.dot(pr.astype(BF16), v, preferred_element_type=F32)
        outs.append(o / jnp.sum(pr, axis=-1, keepdims=True))
    o_ref[...] = jnp.where(head_lo, outs[0], outs[1]).astype(o_ref.dtype)


def _ctx_attention(p, naw, batch, ctx_len, n_lat):
    hp = naw // LANES
    blk0 = n_lat // ctx_len
    return pl.pallas_call(
        _ctx_attn_kernel,
        out_shape=jax.ShapeDtypeStruct((batch * ctx_len, naw), BF16),
        grid=(batch, hp),
        in_specs=[pl.BlockSpec((ctx_len, LANES), lambda b, g: (blk0 + b, g)),
                  pl.BlockSpec((ctx_len, LANES), lambda b, g: (blk0 + b, hp + g)),
                  pl.BlockSpec((ctx_len, LANES), lambda b, g: (blk0 + b, 2 * hp + g))],
        out_specs=pl.BlockSpec((ctx_len, LANES), lambda b, g: (b, g)),
        compiler_params=_cparams(("parallel", "parallel")),
        name="ctx_attention",
    )(p, p, p)


def _s5_operators(a_re, a_im, b_re, b_im, c_re, c_im, log_dt):
    ell = S5_CHUNK
    lam = lax.complex(a_re.astype(F32), a_im.astype(F32))
    dt = jnp.exp(log_dt.astype(F32))[..., None]
    a_bar = jnp.exp(lam * dt)
    b_bar = ((a_bar - 1.0) / lam)[..., None] * lax.complex(b_re.astype(F32), b_im.astype(F32))
    cm = lax.complex(c_re.astype(F32), c_im.astype(F32))
    kpow = jnp.arange(ell + 1, dtype=F32)
    apow = jnp.exp((lam * dt)[..., None] * kpow)
    g = lam.shape[1]
    pdim = lam.shape[2]
    ch = b_re.shape[-1]

    ktau = jnp.real(jnp.einsum('dgcp,dgpt,dgpi->dgtci', cm, apow[..., :ell], b_bar))
    s_idx = jnp.arange(ell)[:, None]
    t_idx = jnp.arange(ell)[None, :]
    ms, bincs, cxs = [], [], []
    for d in range(2):
        delta = (t_idx - s_idx) if d == 0 else (s_idx - t_idx)
        kk = ktau[d][:, jnp.clip(delta, 0, ell - 1)]
        kk = jnp.where((delta >= 0)[None, :, :, None, None], kk, 0.0)
        ms.append(kk.transpose(0, 1, 4, 2, 3).reshape(g, ell * ch, ell * ch))
        e_inc = (ell - 1 - jnp.arange(ell)) if d == 0 else jnp.arange(ell)
        binc = apow[d][:, :, e_inc][:, :, :, None] * b_bar[d][:, :, None, :]
        bincs.append(binc.transpose(0, 2, 3, 1).reshape(g, ell * ch, pdim))
        e_out = (jnp.arange(ell) + 1) if d == 0 else (ell - jnp.arange(ell))
        cw = cm[d][:, None, :, :] * apow[d][:, :, e_out].transpose(0, 2, 1)[:, :, None, :]
        cxs.append(cw.transpose(0, 3, 1, 2).reshape(g, pdim, ell * ch))
    m = jnp.stack(ms)
    binc = jnp.stack(bincs)
    cx = jnp.stack(cxs)
    al = apow[..., ell]

    eye2 = jnp.eye(2, dtype=F32)
    bp = binc.reshape(2, g // 2, 2, ell * ch, pdim)

    def pair_rows(x):
        return jnp.einsum('dngkp,gh->dngkhp', x, eye2).reshape(2, g // 2, 2 * ell * ch, 2 * pdim)

    binc_pair = jnp.concatenate([pair_rows(jnp.real(bp)), pair_rows(jnp.imag(bp))], axis=-1)
    cp = cx.reshape(2, g // 2, 2, pdim, ell * ch)

    def pair_cols(x):
        return jnp.einsum('dngpk,gh->dngphk', x, eye2).reshape(2, g // 2, 2 * pdim, 2 * ell * ch)

    cx_pair = jnp.concatenate([pair_cols(jnp.real(cp)), pair_cols(-jnp.imag(cp))], axis=-2)
    alp = al.reshape(2, g // 2, 2 * pdim)
    al_pair = jnp.concatenate([jnp.real(alp), jnp.imag(alp)], axis=-1)[:, :, None, :]
    return m.astype(BF16), binc_pair.astype(BF16), cx_pair.astype(BF16), al_pair.astype(F32)


def _s5_kernel(x_ref, m_ref, binc_ref, cx_ref, al_ref, y_ref, inc_scr, hs_scr, *, batch, n_ctx, n_lat):
    half = m_ref.shape[-1]
    sw = al_ref.shape[-1] // 2
    x = x_ref[0]
    n_all = n_ctx + n_lat

    for d in range(2):
        inc_scr[d] = jnp.dot(x, binc_ref[d, 0], preferred_element_type=F32)

    def make_step(d):
        ar = al_ref[d, 0, :, 0:sw]
        ai = al_ref[d, 0, :, sw:2 * sw]

        def step(mi, carry):
            hr, hi = carry
            r0 = pl.multiple_of(mi * batch, batch)
            hs_scr[d, pl.ds(r0, batch), 0:sw] = hr
            hs_scr[d, pl.ds(r0, batch), sw:2 * sw] = hi
            ir = inc_scr[d, pl.ds(r0, batch), 0:sw]
            ii = inc_scr[d, pl.ds(r0, batch), sw:2 * sw]
            return ar * hr - ai * hi + ir, ar * hi + ai * hr + ii
        return step

    zero = jnp.zeros((batch, sw), F32)
    lax.fori_loop(0, n_all, make_step(0), (zero, zero))
    bstep = make_step(1)
    carry = lax.fori_loop(0, n_ctx, lambda j, c: bstep(n_ctx - 1 - j, c), (zero, zero))
    lax.fori_loop(0, n_lat, lambda j, c: bstep(n_all - 1 - j, c), carry)

    y = None
    for d in range(2):
        intra = jnp.concatenate(
            [jnp.dot(x[:, 0:half], m_ref[d, 0], preferred_element_type=F32),
             jnp.dot(x[:, half:2 * half], m_ref[d, 1], preferred_element_type=F32)], axis=-1)
        cross = jnp.dot(hs_scr[d].astype(BF16), cx_ref[d, 0], preferred_element_type=F32)
        y = intra + cross if y is None else y + intra + cross
    y_ref[0] = y.astype(y_ref.dtype)


def _s5_scan(xp, m, binc, cx, al, batch, n_ctx, n_lat):
    gp, rows, width = xp.shape
    kern = functools.partial(_s5_kernel, batch=batch, n_ctx=n_ctx, n_lat=n_lat)
    return pl.pallas_call(
        kern,
        out_shape=jax.ShapeDtypeStruct((gp, rows, width), BF16),
        grid=(gp,),
        in_specs=[pl.BlockSpec((1, rows, width), lambda g: (g, 0, 0)),
                  pl.BlockSpec((2, 2, width // 2, width // 2), lambda g: (0, g, 0, 0)),
                  pl.BlockSpec((2, 1, width, binc.shape[-1]), lambda g: (0, g, 0, 0)),
                  pl.BlockSpec((2, 1, cx.shape[-2], width), lambda g: (0, g, 0, 0)),
                  pl.BlockSpec((2, 1, 1, al.shape[-1]), lambda g: (0, g, 0, 0))],
        out_specs=pl.BlockSpec((1, rows, width), lambda g: (g, 0, 0)),
        scratch_shapes=[pltpu.VMEM((2, rows, binc.shape[-1]), F32),
                        pltpu.VMEM((2, rows, binc.shape[-1]), F32)],
        compiler_params=_cparams(("parallel",)),
        name="s5_scan",
    )(xp, m, binc, cx, al)


def _merge_kernel(att_ref, ys_ref, u_ref, d_ref, wg_ref, wo_ref, o_ref):
    naw = att_ref.shape[-1]
    y = ys_ref[...].astype(F32) + d_ref[...] * u_ref[...].astype(F32)
    z = jax.nn.gelu(y)
    gate = jnp.dot(z.astype(BF16), wg_ref[...], preferred_element_type=F32)
    zg = (z * jax.nn.sigmoid(gate)).astype(BF16)
    out = (jnp.dot(att_ref[...], wo_ref[0:naw, :], preferred_element_type=F32)
           + jnp.dot(zg, wo_ref[naw:, :], preferred_element_type=F32))
    o_ref[...] = out.astype(o_ref.dtype)


def _merge(att, ys, p, d_skip, w_glu, w_out):
    nt, naw = att.shape
    sw = ys.shape[1]
    d = w_out.shape[1]
    tm = min(ROW_TILE, nt)
    u_blk = (p.shape[1] - sw) // sw
    return pl.pallas_call(
        _merge_kernel,
        out_shape=jax.ShapeDtypeStruct((nt, d), BF16),
        grid=(nt // tm,),
        in_specs=[pl.BlockSpec((tm, naw), lambda i: (i, 0)),
                  pl.BlockSpec((tm, sw), lambda i: (i, 0)),
                  pl.BlockSpec((tm, sw), lambda i: (i, u_blk)),
                  pl.BlockSpec((1, sw), lambda i: (0, 0)),
                  pl.BlockSpec((sw, sw), lambda i: (0, 0)),
                  pl.BlockSpec((naw + sw, d), lambda i: (0, 0))],
        out_specs=pl.BlockSpec((tm, d), lambda i: (i, 0)),
        compiler_params=_cparams(("parallel",)),
        name="merge_outproj",
    )(att, ys, p, d_skip, w_glu, w_out)


def _tail_kernel(x_ref, y_ref, mod_ref, nw_ref, rw_ref, rb_ref, xo_ref, h_ref, idx_ref, gate_ref):
    y = y_ref[...].astype(F32)
    xn = x_ref[...] + mod_ref[0, 2:3, :] * (_rms(y) * nw_ref[1:2, :])
    xo_ref[...] = xn
    h = (_rms(xn) * nw_ref[2:3, :]) * (1.0 + mod_ref[0, 4:5, :]) + mod_ref[0, 3:4, :]
    h_ref[...] = h
    logits = lax.dot_general(rw_ref[...], h.astype(BF16), (((1,), (1,)), ((), ())),
                             preferred_element_type=F32) + rb_ref[...]
    n_exp = logits.shape[0]
    eidx = lax.broadcasted_iota(jnp.int32, logits.shape, 0)
    vals, idxs = [], []
    cur = logits
    for _ in range(TOP_K):
        m = jnp.max(cur, axis=0, keepdims=True)
        sel = jnp.min(jnp.where(cur == m, eidx, n_exp), axis=0, keepdims=True)
        vals.append(m)
        idxs.append(sel)
        cur = jnp.where(eidx == sel, -jnp.inf, cur)
    es = [jnp.exp(v - vals[0]) for v in vals]
    tot = es[0] + es[1] + es[2] + es[3]
    idx_ref[...] = jnp.concatenate(idxs, axis=0)
    gate_ref[...] = jnp.concatenate([e / tot for e in es], axis=0)


def _tail(x, y, mod, nw, rw_t, rb, mod_row):
    nt, d = y.shape
    n_exp = rw_t.shape[0]
    tm = min(ROW_TILE, nt)
    return pl.pallas_call(
        _tail_kernel,
        out_shape=(jax.ShapeDtypeStruct((nt, d), F32),
                   jax.ShapeDtypeStruct((nt, d), F32),
                   jax.ShapeDtypeStruct((TOP_K, nt), jnp.int32),
                   jax.ShapeDtypeStruct((TOP_K, nt), F32)),
        grid=(nt // tm,),
        in_specs=[pl.BlockSpec((tm, d), lambda i: (i, 0)),
                  pl.BlockSpec((tm, d), lambda i: (i, 0)),
                  pl.BlockSpec((1, 6, d), lambda i: (mod_row(i, tm), 0, 0)),
                  pl.BlockSpec((4, d), lambda i: (0, 0)),
                  pl.BlockSpec((n_exp, d), lambda i: (0, 0)),
                  pl.BlockSpec((n_exp, 1), lambda i: (0, 0))],
        out_specs=(pl.BlockSpec((tm, d), lambda i: (i, 0)),
                   pl.BlockSpec((tm, d), lambda i: (i, 0)),
                   pl.BlockSpec((TOP_K, tm), lambda i: (0, i)),
                   pl.BlockSpec((TOP_K, tm), lambda i: (0, i))),
        compiler_params=_cparams(("parallel",)),
        name="tail_router",
    )(x, y, mod, nw, rw_t, rb)


def _route(idx_t, n_exp, tm):
    n_tok = idx_t.shape[1]
    e_flat = idx_t.reshape(-1)
    n_asg = e_flat.shape[0]
    n_tiles = -(-(n_asg + n_exp * (tm - 1)) // tm)
    order = jnp.argsort(e_flat, stable=True).astype(jnp.int32)
    cnt = jnp.sum((e_flat[:, None] == jnp.arange(n_exp)[None, :]).astype(jnp.int32), axis=0)
    cstart = jnp.cumsum(cnt) - cnt
    pcnt = ((cnt + tm - 1) // tm) * tm
    pend = jnp.cumsum(pcnt)
    poff = pend - pcnt
    tile_start = jnp.arange(n_tiles, dtype=jnp.int32) * tm
    tile_exp = jnp.minimum(jnp.searchsorted(pend, tile_start, side='right'), n_exp - 1).astype(jnp.int32)
    tile_rows = jnp.clip(cnt[tile_exp] - (tile_start - poff[tile_exp]), 0, tm).astype(jnp.int32)
    r = jnp.arange(tm, dtype=jnp.int32)[None, :]
    valid = r < tile_rows[:, None]
    q = cstart[tile_exp][:, None] + (tile_start - poff[tile_exp])[:, None] + r
    asg = order[jnp.clip(q, 0, n_asg - 1)]
    src_tok = jnp.where(valid, asg % n_tok, 0).astype(jnp.int32).reshape(n_tiles, 1, tm)
    dst_row = jnp.where(valid, asg, 0).astype(jnp.int32).reshape(n_tiles, 1, tm)
    return tile_exp, tile_rows, src_tok, dst_row


def _ffn_kernel(te_ref, nr_ref, src_ref, srcn_ref, dst_ref, h_hbm, w1_ref, b1_ref, w2_ref, b2_ref,
                y_hbm, xbuf, ybuf, gsem, ssem, *, tm, n_tiles):
    t = pl.program_id(0)
    slot = t % 2
    other = 1 - slot
    ff = w2_ref.shape[1]

    def gather_copy(tok, row, sl):
        return pltpu.make_async_copy(h_hbm.at[pl.ds(tok, 1)], xbuf.at[sl, pl.ds(row, 1)], gsem.at[sl])

    def scatter_copy(row, dst, sl):
        return pltpu.make_async_copy(ybuf.at[sl, pl.ds(row, 1)], y_hbm.at[pl.ds(dst, 1)], ssem.at[sl])

    def start_gather(idx_ref, sl):
        def body(i, c):
            gather_copy(idx_ref[0, 0, i], i, sl).start()
            return c
        lax.fori_loop(0, tm, body, 0)

    def wait_scatter(n, sl):
        def body(i, c):
            scatter_copy(0, 0, sl).wait()
            return c
        lax.fori_loop(0, n, body, 0)

    @pl.when(t == 0)
    def _():
        start_gather(src_ref, 0)

    @pl.when(t + 1 < n_tiles)
    def _():
        start_gather(srcn_ref, other)

    def gwait(i, c):
        gather_copy(0, 0, slot).wait()
        return c
    lax.fori_loop(0, tm, gwait, 0)

    @pl.when(t >= 2)
    def _():
        wait_scatter(nr_ref[t - 2], slot)

    n_rows = nr_ref[t]

    @pl.when(n_rows > 0)
    def _():
        x = xbuf[slot].astype(BF16)
        hu = jnp.dot(x, w1_ref[0], preferred_element_type=F32) + b1_ref[0]
        x_glu = jnp.minimum(hu[:, 0:ff], SWIGLU_LIMIT)
        x_lin = jnp.clip(hu[:, ff:2 * ff], -SWIGLU_LIMIT, SWIGLU_LIMIT)
        act = x_glu * jax.nn.sigmoid(SWIGLU_ALPHA * x_glu) * (x_lin + 1.0)
        ybuf[slot] = jnp.dot(act.astype(BF16), w2_ref[0], preferred_element_type=F32) + b2_ref[0]

    def sbody(i, c):
        scatter_copy(i, dst_ref[0, 0, i], slot).start()
        return c
    lax.fori_loop(0, n_rows, sbody, 0)

    @pl.when(t == n_tiles - 1)
    def _():
        wait_scatter(n_rows, slot)
        if n_tiles > 1:
            wait_scatter(nr_ref[t - 1], other)


def _expert_ffn(h, tile_exp, tile_rows, src_tok, dst_row, w1, b1, w2, b2):
    n_tok, d = h.shape
    n_tiles, _, tm = src_tok.shape
    ff = w2.shape[1]
    kern = functools.partial(_ffn_kernel, tm=tm, n_tiles=n_tiles)
    grid_spec = pltpu.PrefetchScalarGridSpec(
        num_scalar_prefetch=2,
        grid=(n_tiles,),
        in_specs=[pl.BlockSpec((1, 1, tm), lambda t, te, nr: (t, 0, 0), memory_space=pltpu.SMEM),
                  pl.BlockSpec((1, 1, tm), lambda t, te, nr: (jnp.minimum(t + 1, n_tiles - 1), 0, 0),
                               memory_space=pltpu.SMEM),
                  pl.BlockSpec((1, 1, tm), lambda t, te, nr: (t, 0, 0), memory_space=pltpu.SMEM),
                  pl.BlockSpec(memory_space=pl.ANY),
                  pl.BlockSpec((1, d, 2 * ff), lambda t, te, nr: (te[t], 0, 0)),
                  pl.BlockSpec((1, 1, 2 * ff), lambda t, te, nr: (te[t], 0, 0)),
                  pl.BlockSpec((1, ff, d), lambda t, te, nr: (te[t], 0, 0)),
                  pl.BlockSpec((1, 1, d), lambda t, te, nr: (te[t], 0, 0))],
        out_specs=pl.BlockSpec(memory_space=pl.ANY),
        scratch_shapes=[pltpu.VMEM((2, tm, d), F32),
                        pltpu.VMEM((2, tm, d), F32),
                        pltpu.SemaphoreType.DMA((2,)),
                        pltpu.SemaphoreType.DMA((2,))],
    )
    return pl.pallas_call(
        kern,
        out_shape=jax.ShapeDtypeStruct((TOP_K * n_tok, d), F32),
        grid_spec=grid_spec,
        compiler_params=_cparams(("arbitrary",)),
        name="expert_ffn",
    )(tile_exp, tile_rows, src_tok, src_tok, dst_row, h, w1, b1, w2, b2)


def _combine_kernel(y0_ref, y1_ref, y2_ref, y3_ref, g_ref, x_ref, mod_ref, nw_ref, o_ref):
    g = g_ref[...]
    y = (g[:, 0:1] * y0_ref[...] + g[:, 1:2] * y1_ref[...]
         + g[:, 2:3] * y2_ref[...] + g[:, 3:4] * y3_ref[...])
    o_ref[...] = x_ref[...] + mod_ref[0, 5:6, :] * (_rms(y) * nw_ref[3:4, :])


def _combine(y4, gates, x, mod, nw, mod_row):
    nt, d = x.shape
    tm = min(ROW_TILE, nt)
    nb = nt // tm
    yspec = [pl.BlockSpec((tm, d), (lambda i, k=k: (k * nb + i, 0))) for k in range(TOP_K)]
    return pl.pallas_call(
        _combine_kernel,
        out_shape=jax.ShapeDtypeStruct((nt, d), F32),
        grid=(nb,),
        in_specs=yspec + [pl.BlockSpec((tm, TOP_K), lambda i: (i, 0)),
                          pl.BlockSpec((tm, d), lambda i: (i, 0)),
                          pl.BlockSpec((1, 6, d), lambda i: (mod_row(i, tm), 0, 0)),
                          pl.BlockSpec((4, d), lambda i: (0, 0))],
        out_specs=pl.BlockSpec((tm, d), lambda i: (i, 0)),
        compiler_params=_cparams(("parallel",)),
        name="moe_combine",
    )(y4, y4, y4, y4, gates, x, mod, nw)


def _moe_block(x1, h2, idx_t, gate_t, mod, nw, mod_row, w1, b1, w2, b2):
    n_exp = w1.shape[0]
    tm = EXPERT_TM
    tile_exp, tile_rows, src_tok, dst_row = _route(idx_t, n_exp, tm)
    y4 = _expert_ffn(h2, tile_exp, tile_rows, src_tok, dst_row, w1, b1, w2, b2)
    return _combine(y4, gate_t.T, x1, mod, nw, mod_row)


def _hgrn_kernel(q_ref, f_ref, v_ref, fc_ref, vc_ref, cos_ref, sin_ref, lb_ref, o_ref,
                 st_scr, acc_scr, *, n_lat_tiles, n_ctx_tiles):
    tile = HG_TILE
    ch = HG_CHUNK
    n_ch = tile // ch
    dk = q_ref.shape[-1]
    hd = dk // 2
    rid = lax.broadcasted_iota(jnp.int32, (tile, tile), 0)
    cid = lax.broadcasted_iota(jnp.int32, (tile, tile), 1)
    same = (rid // ch) == (cid // ch)
    lb = lb_ref[0]

    def tile_pass(d, fblk, vblk, qblk, cs, t0, latent):
        tri = (cid <= rid) if d == 0 else (cid >= rid)
        mask = jnp.logical_and(same, tri)
        cum = jnp.where(mask, 1.0, 0.0).astype(BF16)
        fl = fblk[:, d * hd:(d + 1) * hd].astype(F32)
        fl = jnp.concatenate([fl, fl], axis=-1)
        f = lb + (1.0 - lb) * jax.nn.sigmoid(fl)
        k = 1.0 - f
        logf = jnp.log(f)
        lf_hi = logf.astype(BF16)
        lf_lo = (logf - lf_hi.astype(F32)).astype(BF16)
        bsum = (jnp.dot(cum, lf_hi, preferred_element_type=F32)
                + jnp.dot(cum, lf_lo, preferred_element_type=F32))
        if latent:
            cosd, sind = cs
            k = k * (cosd + sind)
        v = vblk.astype(BF16)
        eb = jnp.exp(bsum)
        outs = None
        if latent:
            qr = qblk.astype(F32) * (dk ** -0.5)
            qr = qr * cosd + pltpu.roll(qr, hd, axis=1) * sind
            qb = (qr * eb).astype(BF16)
            kb = (k * jnp.exp(-bsum)).astype(BF16)
            att = lax.dot_general(qb, kb, (((1,), (1,)), ((), ())), preferred_element_type=F32)
            att = jnp.where(mask, att, 0.0).astype(BF16)
            outs = jnp.dot(att, v, preferred_element_type=F32)
        pieces = []
        for c in range(n_ch):
            ci = c if d == 0 else n_ch - 1 - c
            lo = ci * ch
            end = lo + ch - 1 if d == 0 else lo
            b_end = bsum[end:end + 1, :]
            kd = (k[lo:lo + ch] * jnp.exp(b_end - bsum[lo:lo + ch])).astype(BF16)
            st = st_scr[...]
            if latent:
                oc = lax.dot_general(qb[lo:lo + ch], st.astype(BF16), (((1,), (1,)), ((), ())),
                                     preferred_element_type=F32)
                pieces.append((ci, oc))
            ds_t = lax.dot_general(v[lo:lo + ch], kd, (((0,), (0,)), ((), ())),
                                   preferred_element_type=F32)
            st_scr[...] = st * jnp.exp(b_end) + ds_t
        if latent:
            pieces.sort(key=lambda pc: pc[0])
            outs = outs + jnp.concatenate([pc[1] for pc in pieces], axis=0)
        return outs

    for d in range(2):
        st_scr[...] = jnp.zeros_like(st_scr)

        def ctx_body(j, c, d=d):
            jj = j if d == 0 else n_ctx_tiles - 1 - j
            r0 = pl.multiple_of(jj * tile, tile)
            tile_pass(d, fc_ref[pl.ds(r0, tile), :], vc_ref[pl.ds(r0, tile), :], None, None, r0, False)
            return c
        lax.fori_loop(0, n_ctx_tiles, ctx_body, 0)

        def lat_body(j, c, d=d):
            jj = j if d == 0 else n_lat_tiles - 1 - j
            r0 = pl.multiple_of(jj * tile, tile)
            cs = (cos_ref[pl.ds(r0, tile), :], sin_ref[pl.ds(r0, tile), :])
            o = tile_pass(d, f_ref[pl.ds(r0, tile), :], v_ref[pl.ds(r0, tile), :],
                          q_ref[pl.ds(r0, tile), :], cs, r0, True)
            if d == 0:
                acc_scr[pl.ds(r0, tile), :] = o
            else:
                tot = acc_scr[pl.ds(r0, tile), :] + o
                o_ref[pl.ds(r0, tile), :] = _rms(tot).astype(o_ref.dtype)
            return c
        lax.fori_loop(0, n_lat_tiles, lat_body, 0)


def _hgrn(p, cosd, sind, lbd, batch, seq, ctx_len, n_lat, d_model):
    heads = HG_HEADS
    dk = d_model // heads
    cblk0 = n_lat // ctx_len
    kern = functools.partial(_hgrn_kernel, n_lat_tiles=seq // HG_TILE, n_ctx_tiles=ctx_len // HG_TILE)
    return pl.pallas_call(
        kern,
        out_shape=jax.ShapeDtypeStruct((n_lat, d_model), BF16),
        grid=(batch, heads),
        in_specs=[pl.BlockSpec((seq, dk), lambda b, h: (b, h)),
                  pl.BlockSpec((seq, dk), lambda b, h: (b, heads + h)),
                  pl.BlockSpec((seq, dk), lambda b, h: (b, 2 * heads + h)),
                  pl.BlockSpec((ctx_len, dk), lambda b, h: (cblk0 + b, heads + h)),
                  pl.BlockSpec((ctx_len, dk), lambda b, h: (cblk0 + b, 2 * heads + h)),
                  pl.BlockSpec((seq, dk), lambda b, h: (0, 0)),
                  pl.BlockSpec((seq, dk), lambda b, h: (0, 0)),
                  pl.BlockSpec((1, 1, dk), lambda b, h: (h, 0, 0))],
        out_specs=pl.BlockSpec((seq, dk), lambda b, h: (b, h)),
        scratch_shapes=[pltpu.VMEM((dk, dk), F32), pltpu.VMEM((seq, dk), F32)],
        compiler_params=_cparams(("parallel", "parallel")),
        name="hgrn2",
    )(p, p, p, p, p, cosd, sind, lbd)


def _readout_kernel(o_ref, g_ref, gw_ref, w_ref, y_ref):
    g = g_ref[...].astype(F32)
    a = o_ref[...].astype(F32) * gw_ref[...] * (g * jax.nn.sigmoid(g))
    y_ref[...] = jnp.dot(a.astype(BF16), w_ref[...], preferred_element_type=F32).astype(y_ref.dtype)


def _readout(o, p, gw, w_out):
    nl, d = o.shape
    tm = min(ROW_TILE, nl)
    return pl.pallas_call(
        _readout_kernel,
        out_shape=jax.ShapeDtypeStruct((nl, d), BF16),
        grid=(nl // tm,),
        in_specs=[pl.BlockSpec((tm, d), lambda i: (i, 0)),
                  pl.BlockSpec((tm, d), lambda i: (i, 3)),
                  pl.BlockSpec((1, d), lambda i: (0, 0)),
                  pl.BlockSpec((d, d), lambda i: (0, 0))],
        out_specs=pl.BlockSpec((tm, d), lambda i: (i, 0)),
        compiler_params=_cparams(("parallel",)),
        name="readout_outproj",
    )(o, p, gw, w_out)


def _odd_in_weight(w_in, d_model):
    heads = HG_HEADS
    dk = d_model // heads
    hd = dk // 2
    wq = w_in[:, :d_model].reshape(-1, heads, hd, 2).transpose(0, 1, 3, 2).reshape(-1, d_model)
    wf = w_in[:, d_model:2 * d_model].reshape(-1, 2, heads, hd).transpose(0, 2, 1, 3).reshape(-1, d_model)
    return jnp.concatenate([wq, wf, w_in[:, 2 * d_model:]], axis=1).astype(BF16)


def _rope_tables(seq, dk):
    t = jnp.arange(seq)
    row = (t // GRID_W).astype(F32)
    col = (t % GRID_W).astype(F32)
    n = dk // 4
    inv = ROPE_BASE ** (-jnp.arange(n, dtype=F32) / n)
    ang = jnp.concatenate([row[:, None] * inv, col[:, None] * inv], axis=-1)
    cos, sin = jnp.cos(ang), jnp.sin(ang)
    return jnp.concatenate([cos, cos], axis=-1), jnp.concatenate([-sin, sin], axis=-1)


def _expert_weights(w1, b1, w2, b2):
    w1p = jnp.concatenate([w1[..., 0::2], w1[..., 1::2]], axis=-1).astype(BF16)
    b1p = jnp.concatenate([b1[..., 0::2], b1[..., 1::2]], axis=-1)[:, None, :].astype(F32)
    return w1p, b1p, w2.astype(BF16), b2[:, None, :].astype(F32)


def kernel(x, c, ctx, c_ctx, w_mod, b_mod, norm_w, ev_w_in, ev_w_out, na_rpb, s5_a_re, s5_a_im, s5_b_re,
           s5_b_im, s5_c_re, s5_c_im, s5_log_dt, s5_d, s5_w_glu, od_w_in, od_w_out, hg_lb_logit, hg_norm_w,
           router_w, router_b, exp_w1, exp_b1, exp_w2, exp_b2):
    batch, seq, d = x.shape
    ctx_len = ctx.shape[1]
    n_lat = batch * seq
    n_ctx = batch * ctx_len
    depth = w_mod.shape[0]
    assert depth == 2, "one even (attention + S5) layer followed by one odd (HGRN2) layer"

    def mod_row(i, tm):
        r = i * tm
        return jnp.where(r < n_lat, r // seq, batch)

    xs = jnp.concatenate([x.reshape(n_lat, d), ctx.reshape(n_ctx, d)], axis=0)

    cc = jnp.zeros((16, d), F32).at[:batch].set(c).at[batch].set(c_ctx)
    mods = [_mod_vectors(cc, w_mod[l], b_mod[l]).reshape(16, 6, d) for l in range(depth)]

    naw = na_rpb.shape[1] * NA_HEAD_DIM
    sw = d - naw
    groups = sw // S5_GROUP
    p0 = _inproj(xs, mods[0], norm_w[0, 0:1], ev_w_in[0].astype(BF16), mod_row)

    ztab = _na_bias_table(na_rpb[0])
    att_l = _na_latent(p0, ztab, batch, seq, ctx_len, n_lat)
    att_c = _ctx_attention(p0, naw, batch, ctx_len, n_lat)
    att = jnp.concatenate([att_l, att_c], axis=0)

    ell = S5_CHUNK
    u = p0[:, 3 * naw:]
    ncl, ncc = seq // ell, ctx_len // ell

    def to_chunks(ub, nchunk):
        ub = ub.reshape(batch, nchunk, ell, groups // 2, 2, S5_GROUP)
        return ub.transpose(3, 1, 0, 4, 2, 5).reshape(groups // 2, nchunk, batch, 2 * ell * S5_GROUP)

    xp = jnp.concatenate([to_chunks(u[n_lat:], ncc), to_chunks(u[:n_lat], ncl)], axis=1)
    xp = xp.reshape(groups // 2, (ncc + ncl) * batch, 2 * ell * S5_GROUP)
    m_op, binc_op, cx_op, al_op = _s5_operators(s5_a_re[0], s5_a_im[0], s5_b_re[0], s5_b_im[0],
                                                s5_c_re[0], s5_c_im[0], s5_log_dt[0])
    yp = _s5_scan(xp, m_op, binc_op, cx_op, al_op, batch, ncc, ncl)
    yp = yp.reshape(groups // 2, ncc + ncl, batch, 2, ell, S5_GROUP)

    def from_chunks(yb, nchunk):
        return yb.transpose(2, 1, 4, 0, 3, 5).reshape(batch * nchunk * ell, sw)

    ys = jnp.concatenate([from_chunks(yp[:, ncc:], ncl), from_chunks(yp[:, :ncc], ncc)], axis=0)

    y0 = _merge(att, ys, p0, s5_d[0].reshape(1, sw).astype(F32), s5_w_glu[0].astype(BF16),
                ev_w_out[0].astype(BF16))
    x1, h2, idx_t, gate_t = _tail(xs, y0, mods[0], norm_w[0], router_w[0].T.astype(BF16),
                                  router_b[0].reshape(-1, 1).astype(F32), mod_row)
    ew = _expert_weights(exp_w1[0], exp_b1[0], exp_w2[0], exp_b2[0])
    x2 = _moe_block(x1, h2, idx_t, gate_t, mods[0], norm_w[0], mod_row, *ew)

    p1 = _inproj(x2, mods[1], norm_w[1, 0:1], _odd_in_weight(od_w_in[0], d), mod_row)
    lb_all = jnp.cumsum(jax.nn.softmax(hg_lb_logit.astype(F32), axis=0), axis=0)
    lb = (lb_all - lb_all[0])[1]
    dk = d // HG_HEADS
    lbh = lb.reshape(HG_HEADS, 1, dk // 2)
    lbd = jnp.concatenate([lbh, lbh], axis=-1)
    cosd, sind = _rope_tables(seq, dk)
    o = _hgrn(p1, cosd, sind, lbd, batch, seq, ctx_len, n_lat, d)
    gw = hg_norm_w[0].reshape(1, d).astype(F32)
    y1 = _readout(o, p1, gw, od_w_out[0].astype(BF16))

    def mod_row_lat(i, tm):
        return (i * tm) // seq

    x3, h3, idx_t1, gate_t1 = _tail(x2, y1, mods[1], norm_w[1], router_w[1].T.astype(BF16),
                                    router_b[1].reshape(-1, 1).astype(F32), mod_row_lat)
    ew1 = _expert_weights(exp_w1[1], exp_b1[1], exp_w2[1], exp_b2[1])
    x4 = _moe_block(x3, h3, idx_t1, gate_t1, mods[1], norm_w[1], mod_row_lat, *ew1)
    return x4.reshape(batch, seq, d)
```
